```python
import math
import jax, jax.numpy as jnp
from jax import lax
import numpy as np

D_MODEL = 1024
BATCH = 4
SEQ = 8192
DEPTH = 4
DEC_BATCH = 16
DEC_SEQ = 32
PAST_LEN = 4096

CHUNK = 64
N_MIXERS = 2
N_RET_LAYERS = (DEPTH + 1) // 2
N_GMLP_LAYERS = DEPTH // 2
MEM_W = D_MODEL // 4
MEM_HEADS = 4
MEM_DH = MEM_W // MEM_HEADS
N_MEM = 256
RET_HEADS = 4
RET_V_W = D_MODEL - MEM_W
RET_DV = RET_V_W // RET_HEADS
RET_DK = RET_DV // 2
RET_QK_W = RET_HEADS * RET_DK
RET_IN_W = 2 * RET_QK_W + 2 * RET_V_W + MEM_W
ROPE_BASE = 10000.0
GMLP_W = D_MODEL - MEM_W
GMLP_GROUPS = 4
GMLP_GD = GMLP_W // GMLP_GROUPS
GMLP_CHUNK = 128
GMLP_IN_W = 2 * GMLP_W + MEM_W
MIX_W = D_MODEL
D_FF = ((8 * D_MODEL // 3) + 127) // 128 * 128
EPS = 1e-6

kernel_name = "hybrid_retention_gmlp_memory_stream_step"


def rms_norm(x, g):
    xf = x.astype(jnp.float32)
    y = xf * lax.rsqrt(jnp.mean(xf * xf, axis=-1, keepdims=True) + EPS)
    return (y * g.astype(jnp.float32)).astype(x.dtype)


def layer_norm(x, g, b):
    xf = x.astype(jnp.float32)
    mu = jnp.mean(xf, axis=-1, keepdims=True)
    xc = xf - mu
    var = jnp.mean(xc * xc, axis=-1, keepdims=True)
    return xc * lax.rsqrt(var + EPS) * g.astype(jnp.float32) + b.astype(jnp.float32)


def add_post_norm(x, y, g, scale):
    return x + (scale * rms_norm(y, g)).astype(x.dtype)


def swiglu(h, w_in, w_out):
    gate, up = jnp.split(h @ w_in, 2, axis=-1)
    return (jax.nn.silu(gate) * up) @ w_out


def rotary(x, pos):
    xf = x.astype(jnp.float32)
    half = xf.shape[-1] // 2
    inv = 1.0 / (ROPE_BASE ** (jnp.arange(half, dtype=jnp.float32) / half))
    ang = pos.astype(jnp.float32)[:, None] * inv[None, :]
    cos = jnp.cos(ang)[None, :, None, :]
    sin = jnp.sin(ang)[None, :, None, :]
    x1, x2 = xf[..., :half], xf[..., half:]
    return jnp.concatenate([x1 * cos - x2 * sin, x2 * cos + x1 * sin], axis=-1)


def retention(q, k, v, s0):
    B, T, H, _ = q.shape
    C = min(T, CHUNK)
    n = T // C
    log_g = jnp.log(1.0 - 2.0 ** (-5.0 - jnp.arange(RET_HEADS, dtype=jnp.float32)))
    idx = jnp.arange(C, dtype=jnp.float32)
    diff = idx[:, None] - idx[None, :]
    decay = jnp.where(diff >= 0, jnp.exp(log_g[:, None, None] * jnp.maximum(diff, 0.0)), 0.0)
    q_dec = jnp.exp(log_g[:, None] * (idx + 1.0))[None, :, :, None]
    k_dec = jnp.exp(log_g[:, None] * (C - 1.0 - idx))[None, :, :, None]
    s_dec = jnp.exp(log_g * C)[None, :, None, None]

    def to_chunks(a):
        return a.reshape(B, n, C, H, a.shape[-1]).transpose(1, 0, 3, 2, 4)

    def step(s, qkv):
        qc, kc, vc = qkv
        att = jnp.einsum('bhid,bhjd->bhij', qc, kc) * decay
        o = jnp.einsum('bhij,bhje->bhie', att, vc) + jnp.einsum('bhid,bhde->bhie', qc, s) * q_dec
        s = s * s_dec + jnp.einsum('bhjd,bhje->bhde', kc * k_dec, vc)
        return s, o

    s_fin, o = lax.scan(step, s0, (to_chunks(q), to_chunks(k), to_chunks(v)))
    o = o.transpose(1, 0, 3, 2, 4).reshape(B, T, H, v.shape[-1])
    return o, s_fin


def spatial_gate(v, w_s, b_s):
    B, T, _ = v.shape
    L = min(T, GMLP_CHUNK)
    n = T // L
    w = jnp.tril(w_s[:, :L, :L].astype(jnp.float32))
    vr = v.reshape(B, n, L, GMLP_GROUPS, GMLP_GD)
    out = jnp.einsum('gpq,bnqgd->bnpgd', w, vr) + b_s[:, :L].T.astype(jnp.float32)[None, None, :, :, None]
    return out.reshape(B, T, GMLP_W)


def memory_attend(q, k, v):
    B, T = q.shape[:2]
    s = jnp.einsum('bthd,bmhd->bhtm', q.astype(jnp.float32), k.astype(jnp.float32)) * (MEM_DH ** -0.5)
    p = jax.nn.softmax(s, axis=-1)
    return jnp.einsum('bhtm,bmhd->bthd', p, v.astype(jnp.float32)).reshape(B, T, MEM_W)


def memory_kv(mem, mem_norm, w_mem_kv):
    B, M, _ = mem.shape
    mf = mem.astype(jnp.float32)
    mhat = mf * lax.rsqrt(jnp.mean(mf * mf, axis=-1, keepdims=True) + EPS)
    kv = jnp.einsum('bmd,ld,ldk->lbmk', mhat, mem_norm.astype(jnp.float32), w_mem_kv.astype(jnp.float32))
    k, v = jnp.split(kv, 2, axis=-1)
    return (k.reshape(DEPTH, B, M, MEM_HEADS, MEM_DH), v.reshape(DEPTH, B, M, MEM_HEADS, MEM_DH))


def run_trunk(x, pos, s_ret, mem_k, mem_v, norm_g, ffn_w_in, ffn_w_out, ret_w_in, ret_gn,
              gmlp_w_in, gmlp_ln_g, gmlp_ln_b, gmlp_w_s, gmlp_b_s, w_out):
    B, T, _ = x.shape
    new_s, new_v = [], []
    for i in range(DEPTH):
        g = norm_g[i]
        j = i // N_MIXERS
        x = add_post_norm(x, swiglu(rms_norm(x, g[0]), ffn_w_in[i, 0], ffn_w_out[i, 0]), g[1], 0.5)
        h = rms_norm(x, g[2])
        if i % N_MIXERS == 0:
            p = h @ ret_w_in[j]
            q, k, v, gate, xq = jnp.split(
                p, [RET_QK_W, 2 * RET_QK_W, 2 * RET_QK_W + RET_V_W, 2 * RET_QK_W + 2 * RET_V_W], axis=-1)
            q = rotary(q.reshape(B, T, RET_HEADS, RET_DK), pos)
            k = rotary(k.reshape(B, T, RET_HEADS, RET_DK), pos) * (RET_DK ** -0.5)
            v = v.reshape(B, T, RET_HEADS, RET_DV).astype(jnp.float32)
            o, s_fin = retention(q, k, v, s_ret[j].astype(jnp.float32))
            o = rms_norm(o, ret_gn[j].reshape(RET_HEADS, RET_DV)).reshape(B, T, RET_V_W)
            tok = jax.nn.silu(gate.astype(jnp.float32)) * o
            new_s.append(s_fin)
        else:
            p = h @ gmlp_w_in[j]
            u, v, xq = jnp.split(p, [GMLP_W, 2 * GMLP_W], axis=-1)
            u = jax.nn.gelu(u.astype(jnp.float32), approximate=False)
            v = layer_norm(jax.nn.gelu(v.astype(jnp.float32), approximate=False), gmlp_ln_g[j], gmlp_ln_b[j])
            tok = u * spatial_gate(v, gmlp_w_s[j], gmlp_b_s[j])
            new_v.append(v)
        mem_o = memory_attend(xq.reshape(B, T, MEM_HEADS, MEM_DH), mem_k[i], mem_v[i])
        mix = jnp.concatenate([tok, mem_o], axis=-1).astype(x.dtype) @ w_out[i]
        x = add_post_norm(x, mix, g[3], 1.0)
        x = add_post_norm(x, swiglu(rms_norm(x, g[4]), ffn_w_in[i, 1], ffn_w_out[i, 1]), g[5], 0.5)
    return x, new_s, new_v


def setup_inputs(seed: int = 0) -> dict:
    key = jax.random.key(seed)
    ks = jax.random.split(key, 20)
    nrm = jax.random.normal
    f32 = jnp.float32
    return {
        "x_prompt": nrm(ks[0], (BATCH, SEQ, D_MODEL), f32),
        "x_sample": nrm(ks[1], (DEC_BATCH, DEC_SEQ, D_MODEL), f32),
        "state_ret": 0.5 * nrm(ks[2], (N_RET_LAYERS, DEC_BATCH, RET_HEADS, RET_DK, RET_DV), f32),
        "cache_mem_k": nrm(ks[3], (DEPTH, DEC_BATCH, N_MEM, MEM_HEADS, MEM_DH), f32),
        "cache_mem_v": nrm(ks[4], (DEPTH, DEC_BATCH, N_MEM, MEM_HEADS, MEM_DH), f32),
        "mem_prompt": nrm(ks[5], (BATCH, N_MEM, D_MODEL), f32),
        "norm_g": 1.0 + 0.02 * nrm(ks[6], (DEPTH, 6, D_MODEL), f32),
        "mem_norm": 1.0 + 0.02 * nrm(ks[7], (DEPTH, D_MODEL), f32),
        "w_mem_kv": nrm(ks[8], (DEPTH, D_MODEL, 2 * MEM_W), f32) * D_MODEL ** -0.5,
        "ffn_w_in": nrm(ks[9], (DEPTH, 2, D_MODEL, 2 * D_FF), f32) * D_MODEL ** -0.5,
        "ffn_w_out": nrm(ks[10], (DEPTH, 2, D_FF, D_MODEL), f32) * D_FF ** -0.5,
        "ret_w_in": nrm(ks[11], (N_RET_LAYERS, D_MODEL, RET_IN_W), f32) * D_MODEL ** -0.5,
        "ret_gn": 1.0 + 0.02 * nrm(ks[12], (N_RET_LAYERS, RET_V_W), f32),
        "gmlp_w_in": nrm(ks[13], (N_GMLP_LAYERS, D_MODEL, GMLP_IN_W), f32) * D_MODEL ** -0.5,
        "gmlp_ln_g": 1.0 + 0.02 * nrm(ks[14], (N_GMLP_LAYERS, GMLP_W), f32),
        "gmlp_ln_b": 0.02 * nrm(ks[15], (N_GMLP_LAYERS, GMLP_W), f32),
        "gmlp_w_s": nrm(ks[16], (N_GMLP_LAYERS, GMLP_GROUPS, GMLP_CHUNK, GMLP_CHUNK), f32) * GMLP_CHUNK ** -0.5,
        "gmlp_b_s": 1.0 + 0.1 * nrm(ks[17], (N_GMLP_LAYERS, GMLP_GROUPS, GMLP_CHUNK), f32),
        "w_out": nrm(ks[18], (DEPTH, MIX_W, D_MODEL), f32) * MIX_W ** -0.5,
    }


def reference(x_prompt, x_sample, state_ret, cache_mem_k, cache_mem_v, mem_prompt, norm_g, mem_norm, w_mem_kv,
              ffn_w_in, ffn_w_out, ret_w_in, ret_gn, gmlp_w_in, gmlp_ln_g, gmlp_ln_b, gmlp_w_s, gmlp_b_s, w_out):
    bp, tp = x_prompt.shape[0], x_prompt.shape[1]
    mem_k_p, mem_v_p = memory_kv(mem_prompt, mem_norm, w_mem_kv)
    s0 = jnp.zeros((N_RET_LAYERS, bp, RET_HEADS, RET_DK, RET_DV), jnp.float32)
    pos_p = jnp.arange(tp, dtype=jnp.int32)
    y_prompt, s_p, _ = run_trunk(x_prompt, pos_p, s0, mem_k_p, mem_v_p, norm_g, ffn_w_in, ffn_w_out, ret_w_in,
                                 ret_gn, gmlp_w_in, gmlp_ln_g, gmlp_ln_b, gmlp_w_s, gmlp_b_s, w_out)
    ts = x_sample.shape[1]
    pos_s = PAST_LEN + jnp.arange(ts, dtype=jnp.int32)
    y_sample, s_s, v_s = run_trunk(x_sample, pos_s, state_ret, cache_mem_k, cache_mem_v, norm_g, ffn_w_in,
                                   ffn_w_out, ret_w_in, ret_gn, gmlp_w_in, gmlp_ln_g, gmlp_ln_b, gmlp_w_s,
                                   gmlp_b_s, w_out)
    state_ret_prompt = jnp.stack(s_p)
    state_ret_sample = jnp.stack(s_s)
    gmlp_v_sample = jnp.stack(v_s)
    return (y_prompt, y_sample, state_ret_prompt, state_ret_sample, gmlp_v_sample, mem_k_p, mem_v_p)
```

```python
import functools
import math

import jax
import jax.numpy as jnp
from jax import lax
from jax.experimental import pallas as pl
from jax.experimental.pallas import tpu as pltpu

D_MODEL = 1024
DEPTH = 4
CHUNK = 64
MEM_W = 256
MEM_HEADS = 4
MEM_DH = 64
N_MEM = 256
RET_HEADS = 4
RET_DK = 96
RET_DV = 192
ROPE_BASE = 10000.0
GMLP_W = 768
GMLP_GROUPS = 4
GMLP_GD = 192
GMLP_CHUNK = 128
D_FF = 2816
EPS = 1e-6
PAST_LEN = 4096

DK_PAD = 128
DV_PAD = 256
TOK_PAD = RET_HEADS * DV_PAD
HALF = RET_DK // 2
HALF_PAD = DK_PAD // 2

FFN_TILE = 512
FFN_CHUNKS = 2
MIX_TILE = 256
VMEM_LIMIT = 52 * 1024 * 1024

BF16 = jnp.bfloat16
F32 = jnp.float32


def _dot(a, b):
    return jnp.dot(a, b, preferred_element_type=F32)


def _dot_nt(a, b):
    return lax.dot_general(a, b, (((1,), (1,)), ((), ())), preferred_element_type=F32)


def _dot_tn(a, b):
    return lax.dot_general(a, b, (((0,), (0,)), ((), ())), preferred_element_type=F32)


def _rms(x, g):
    return x * lax.rsqrt(jnp.mean(x * x, axis=-1, keepdims=True) + EPS) * g


def _const_spec(shape):
    nd = len(shape)
    return pl.BlockSpec(shape, lambda *_: (0,) * nd, pipeline_mode=pl.Buffered(1))


def _ffn_kernel(x_ref, g_ref, win_ref, wout_ref, o_ref):
    x = x_ref[...]
    h = _rms(x, g_ref[0:1, :]).astype(BF16)
    fc = D_FF // FFN_CHUNKS
    acc = None
    for c in range(FFN_CHUNKS):
        gate = _dot(h, win_ref[:, c * fc:(c + 1) * fc])
        up = _dot(h, win_ref[:, D_FF + c * fc:D_FF + (c + 1) * fc])
        a = (gate * (1.0 / (1.0 + jnp.exp(-gate))) * up).astype(BF16)
        part = _dot(a, wout_ref[c * fc:(c + 1) * fc, :])
        acc = part if acc is None else acc + part
    o_ref[...] = x + 0.5 * _rms(acc, g_ref[1:2, :])


def _ffn_call(x2d, g2, w_in, w_out):
    n = x2d.shape[0]
    tm = min(FFN_TILE, n)
    assert n % tm == 0
    return pl.pallas_call(
        _ffn_kernel,
        grid=(n // tm,),
        in_specs=[
            pl.BlockSpec((tm, D_MODEL), lambda i: (i, 0)),
            _const_spec((2, D_MODEL)),
            _const_spec((D_MODEL, 2 * D_FF)),
            _const_spec((D_FF, D_MODEL)),
        ],
        out_specs=pl.BlockSpec((tm, D_MODEL), lambda i: (i, 0)),
        out_shape=jax.ShapeDtypeStruct((n, D_MODEL), F32),
        compiler_params=pltpu.CompilerParams(
            dimension_semantics=("arbitrary",), vmem_limit_bytes=VMEM_LIMIT),
        name="ffn",
    )(x2d, g2, w_in, w_out)


def _memory_attend(xq, kbd_ref, vbd_ref):
    s = _dot(xq.astype(BF16), kbd_ref[0]) * (MEM_DH ** -0.5)
    ps = []
    for h in range(MEM_HEADS):
        sh = s[:, h * N_MEM:(h + 1) * N_MEM]
        e = jnp.exp(sh - jnp.max(sh, axis=-1, keepdims=True))
        ps.append((e * (1.0 / jnp.sum(e, axis=-1, keepdims=True))).astype(BF16))
    p = jnp.concatenate(ps, axis=-1)
    return _dot(p, vbd_ref[0])


def _mix_out(x, tok, mem_o, wtok_ref, wmem_ref, g_post):
    mix = _dot(tok.astype(BF16), wtok_ref[...]) + _dot(mem_o.astype(BF16), wmem_ref[...])
    return x + _rms(mix, g_post)


def _ret_kernel(x_ref, g_ref, win_ref, cos_ref, sin_ref, gn_ref, s0_ref, kbd_ref, vbd_ref,
                wtok_ref, wmem_ref, o_ref, s_ref, *, chunk, n_chunks):
    t = pl.program_id(1)

    @pl.when(t == 0)
    def _():
        s_ref[...] = s0_ref[...]

    x = x_ref[0]
    h = _rms(x, g_ref[0:1, :]).astype(BF16)
    p = _dot(h, win_ref[...])
    cos = cos_ref[...]
    sin = sin_ref[...]
    qo, ko, vo, go, xo = 0, 4 * DK_PAD, 8 * DK_PAD, 8 * DK_PAD + TOK_PAD, 8 * DK_PAD + 2 * TOK_PAD

    idx = lax.broadcasted_iota(jnp.int32, (chunk, 1), 0).astype(F32)
    row = lax.broadcasted_iota(jnp.int32, (chunk, chunk), 0)
    col = lax.broadcasted_iota(jnp.int32, (chunk, chunk), 1)
    diff = (row - col).astype(F32)

    o_heads = []
    for hd in range(RET_HEADS):
        log_g = math.log(1.0 - 2.0 ** (-5.0 - hd))
        decay = jnp.where(diff >= 0, jnp.exp(log_g * jnp.maximum(diff, 0.0)), 0.0)
        q_dec = jnp.exp(log_g * (idx + 1.0))
        k_dec = jnp.exp(log_g * (chunk - 1.0 - idx))
        s_dec = math.exp(log_g * chunk)
        qh = p[:, qo + hd * DK_PAD:qo + (hd + 1) * DK_PAD]
        kh = p[:, ko + hd * DK_PAD:ko + (hd + 1) * DK_PAD]
        qh = qh * cos + pltpu.roll(qh, HALF_PAD, 1) * sin
        kh = (kh * cos + pltpu.roll(kh, HALF_PAD, 1) * sin) * (RET_DK ** -0.5)
        vh = p[:, vo + hd * DV_PAD:vo + (hd + 1) * DV_PAD]
        o_chunks = []
        for c in range(n_chunks):
            sl = slice(c * chunk, (c + 1) * chunk)
            qc = qh[sl].astype(BF16)
            kc = kh[sl]
            vc = vh[sl].astype(BF16)
            st = s_ref[0, hd]
            att = (_dot_nt(qc, kc.astype(BF16)) * decay).astype(BF16)
            o = _dot(att, vc) + _dot(qc, st.astype(BF16)) * q_dec
            s_ref[0, hd] = st * s_dec + _dot_tn((kc * k_dec).astype(BF16), vc)
            o_chunks.append(o)
        o = o_chunks[0] if n_chunks == 1 else jnp.concatenate(o_chunks, axis=0)
        ms = jnp.sum(o * o, axis=-1, keepdims=True) * (1.0 / RET_DV)
        o_heads.append(o * lax.rsqrt(ms + EPS) * gn_ref[0:1, hd * DV_PAD:(hd + 1) * DV_PAD])
    o_all = jnp.concatenate(o_heads, axis=-1)
    gate = p[:, go:go + TOK_PAD]
    tok = gate * (1.0 / (1.0 + jnp.exp(-gate))) * o_all
    mem_o = _memory_attend(p[:, xo:xo + MEM_W], kbd_ref, vbd_ref)
    o_ref[0] = _mix_out(x, tok, mem_o, wtok_ref, wmem_ref, g_ref[1:2, :])


def _ret_call(x, g2, w_in, cos, sin, gn, s0, kbd, vbd, w_tok, w_mem):
    b, t, _ = x.shape
    tt = min(MIX_TILE, t)
    chunk = tt
    n_in = w_in.shape[1]
    kern = functools.partial(_ret_kernel, chunk=chunk, n_chunks=tt // chunk)
    return pl.pallas_call(
        kern,
        grid=(b, t // tt),
        in_specs=[
            pl.BlockSpec((1, tt, D_MODEL), lambda i, j: (i, j, 0)),
            _const_spec((2, D_MODEL)),
            _const_spec((D_MODEL, n_in)),
            pl.BlockSpec((tt, DK_PAD), lambda i, j: (j, 0)),
            pl.BlockSpec((tt, DK_PAD), lambda i, j: (j, 0)),
            _const_spec((1, TOK_PAD)),
            pl.BlockSpec((1, RET_HEADS, DK_PAD, DV_PAD), lambda i, j: (i, 0, 0, 0)),
            pl.BlockSpec((1, MEM_W, MEM_HEADS * N_MEM), lambda i, j: (i, 0, 0)),
            pl.BlockSpec((1, MEM_HEADS * N_MEM, MEM_W), lambda i, j: (i, 0, 0)),
            _const_spec((TOK_PAD, D_MODEL)),
            _const_spec((MEM_W, D_MODEL)),
        ],
        out_specs=[
            pl.BlockSpec((1, tt, D_MODEL), lambda i, j: (i, j, 0)),
            pl.BlockSpec((1, RET_HEADS, DK_PAD, DV_PAD), lambda i, j: (i, 0, 0, 0)),
        ],
        out_shape=[
            jax.ShapeDtypeStruct((b, t, D_MODEL), F32),
            jax.ShapeDtypeStruct((b, RET_HEADS, DK_PAD, DV_PAD), F32),
        ],
        compiler_params=pltpu.CompilerParams(
            dimension_semantics=("arbitrary", "arbitrary"), vmem_limit_bytes=VMEM_LIMIT),
        name="retention_mixer",
    )(x, g2, w_in, cos, sin, gn, s0, kbd, vbd, w_tok, w_mem)


def _gelu(x):
    return 0.5 * x * (1.0 + lax.erf(x * (2.0 ** -0.5)))


def _gmlp_kernel(x_ref, g_ref, win_ref, ln_ref, ws_ref, bs_ref, kbd_ref, vbd_ref, wtok_ref, wmem_ref,
                 o_ref, *maybe_v_ref, chunk, n_chunks):
    x = x_ref[0]
    h = _rms(x, g_ref[0:1, :]).astype(BF16)
    p = _dot(h, win_ref[...])
    u = _gelu(p[:, 0:TOK_PAD])
    v = _gelu(p[:, TOK_PAD:2 * TOK_PAD])
    lane = lax.broadcasted_iota(jnp.int32, (1, TOK_PAD), 1)
    real = (lane % DV_PAD) < GMLP_GD
    mu = jnp.sum(v, axis=-1, keepdims=True) * (1.0 / GMLP_W)
    vc = jnp.where(real, v - mu, 0.0)
    var = jnp.sum(vc * vc, axis=-1, keepdims=True) * (1.0 / GMLP_W)
    v = vc * lax.rsqrt(var + EPS) * ln_ref[0:1, :] + ln_ref[1:2, :]
    if maybe_v_ref:
        maybe_v_ref[0][0] = v
    vb = v.astype(BF16)
    row = lax.broadcasted_iota(jnp.int32, (chunk, chunk), 0)
    col = lax.broadcasted_iota(jnp.int32, (chunk, chunk), 1)
    groups = []
    for gi in range(GMLP_GROUPS):
        w = jnp.where(row >= col, ws_ref[gi], 0.0).astype(BF16)
        bias = bs_ref[:, gi:gi + 1]
        outs = []
        for c in range(n_chunks):
            vg = vb[c * chunk:(c + 1) * chunk, gi * DV_PAD:(gi + 1) * DV_PAD]
            outs.append(_dot(w, vg) + bias)
        groups.append(outs[0] if n_chunks == 1 else jnp.concatenate(outs, axis=0))
    tok = u * jnp.concatenate(groups, axis=-1)
    mem_o = _memory_attend(p[:, 2 * TOK_PAD:2 * TOK_PAD + MEM_W], kbd_ref, vbd_ref)
    o_ref[0] = _mix_out(x, tok, mem_o, wtok_ref, wmem_ref, g_ref[1:2, :])


def _gmlp_call(x, g2, w_in, ln, ws, bs, kbd, vbd, w_tok, w_mem, want_v):
    b, t, _ = x.shape
    tt = min(MIX_TILE, t)
    chunk = min(GMLP_CHUNK, t)
    n_in = w_in.shape[1]
    kern = functools.partial(_gmlp_kernel, chunk=chunk, n_chunks=tt // chunk)
    out_specs = [pl.BlockSpec((1, tt, D_MODEL), lambda i, j: (i, j, 0))]
    out_shape = [jax.ShapeDtypeStruct((b, t, D_MODEL), F32)]
    if want_v:
        out_specs.append(pl.BlockSpec((1, tt, TOK_PAD), lambda i, j: (i, j, 0)))
        out_shape.append(jax.ShapeDtypeStruct((b, t, TOK_PAD), F32))
    return pl.pallas_call(
        kern,
        grid=(b, t // tt),
        in_specs=[
            pl.BlockSpec((1, tt, D_MODEL), lambda i, j: (i, j, 0)),
            _const_spec((2, D_MODEL)),
            _const_spec((D_MODEL, n_in)),
            _const_spec((2, TOK_PAD)),
            _const_spec((GMLP_GROUPS, chunk, chunk)),
            _const_spec((chunk, GMLP_GROUPS)),
            pl.BlockSpec((1, MEM_W, MEM_HEADS * N_MEM), lambda i, j: (i, 0, 0)),
            pl.BlockSpec((1, MEM_HEADS * N_MEM, MEM_W), lambda i, j: (i, 0, 0)),
            _const_spec((TOK_PAD, D_MODEL)),
            _const_spec((MEM_W, D_MODEL)),
        ],
        out_specs=out_specs,
        out_shape=out_shape,
        compiler_params=pltpu.CompilerParams(
            dimension_semantics=("arbitrary", "arbitrary"), vmem_limit_bytes=VMEM_LIMIT),
        name="gmlp_mixer",
    )(x, g2, w_in, ln, ws, bs, kbd, vbd, w_tok, w_mem)


def _memkv_kernel(mem_ref, norm_ref, w_ref, k_ref, v_ref):
    m = mem_ref[...]
    mhat = m * lax.rsqrt(jnp.mean(m * m, axis=-1, keepdims=True) + EPS)
    kv = _dot((mhat * norm_ref[0]).astype(BF16), w_ref[0])
    k_ref[0] = kv[:, :MEM_W]
    v_ref[0] = kv[:, MEM_W:]


def _memkv_call(mem2d, mem_norm, w_kv):
    n = mem2d.shape[0]
    return pl.pallas_call(
        _memkv_kernel,
        grid=(DEPTH,),
        in_specs=[
            _const_spec((n, D_MODEL)),
            pl.BlockSpec((1, 1, D_MODEL), lambda l: (l, 0, 0)),
            pl.BlockSpec((1, D_MODEL, 2 * MEM_W), lambda l: (l, 0, 0)),
        ],
        out_specs=[pl.BlockSpec((1, n, MEM_W), lambda l: (l, 0, 0))] * 2,
        out_shape=[jax.ShapeDtypeStruct((DEPTH, n, MEM_W), F32)] * 2,
        compiler_params=pltpu.CompilerParams(
            dimension_semantics=("arbitrary",), vmem_limit_bytes=VMEM_LIMIT),
        name="memory_kv",
    )(mem2d, mem_norm.reshape(DEPTH, 1, D_MODEL), w_kv)


def _pad_last(a, n):
    return jnp.pad(a, [(0, 0)] * (a.ndim - 1) + [(0, n - a.shape[-1])])


def _pad_groups(a, real, padded):
    lead = a.shape[:-1]
    g = a.shape[-1] // real
    return _pad_last(a.reshape(lead + (g, real)), padded).reshape(lead + (g * padded,))


def _pad_qk(a):
    lead = a.shape[:-1]
    h = a.shape[-1] // RET_DK
    return _pad_last(a.reshape(lead + (h, 2, HALF)), HALF_PAD).reshape(lead + (h * DK_PAD,))


def _pad_state(s):
    b, h = s.shape[:2]
    s = s.reshape(b, h, 2, HALF, RET_DV)
    s = jnp.pad(s, [(0, 0), (0, 0), (0, 0), (0, HALF_PAD - HALF), (0, DV_PAD - RET_DV)])
    return s.reshape(b, h, DK_PAD, DV_PAD)


def _unpad_state(s):
    b, h = s.shape[:2]
    return s.reshape(b, h, 2, HALF_PAD, DV_PAD)[:, :, :, :HALF, :RET_DV].reshape(b, h, RET_DK, RET_DV)


def _rope_tables(pos):
    inv = 1.0 / (ROPE_BASE ** (jnp.arange(HALF, dtype=F32) / HALF))
    ang = pos.astype(F32)[:, None] * inv[None, :]
    cos = _pad_last(jnp.cos(ang), HALF_PAD)
    sin = _pad_last(jnp.sin(ang), HALF_PAD)
    return jnp.concatenate([cos, cos], axis=-1), jnp.concatenate([-sin, sin], axis=-1)


def _block_diag_kv(k, v):
    b = k.shape[0]
    eye = jnp.eye(MEM_HEADS, dtype=jnp.bool_)
    kt = jnp.transpose(k.astype(BF16), (0, 2, 3, 1))
    kbd = jnp.where(eye[None, :, None, :, None], kt[:, :, :, None, :], 0)
    vb = v.astype(BF16)
    vbd = jnp.where(eye[None, :, None, :, None], vb[:, None, :, :, :], 0)
    return kbd.reshape(b, MEM_W, MEM_HEADS * N_MEM), vbd.reshape(b, MEM_HEADS * N_MEM, MEM_W)


def _prep_weights(norm_g, ffn_w_in, ffn_w_out, ret_w_in, ret_gn, gmlp_w_in, gmlp_ln_g, gmlp_ln_b,
                  gmlp_w_s, gmlp_b_s, w_out):
    qk = 2 * RET_HEADS * RET_DK
    vw = RET_HEADS * RET_DV
    ret_in = jnp.concatenate([
        _pad_qk(ret_w_in[..., :qk]),
        _pad_groups(ret_w_in[..., qk:qk + 2 * vw], RET_DV, DV_PAD),
        ret_w_in[..., qk + 2 * vw:],
    ], axis=-1).astype(BF16)
    gmlp_in = jnp.concatenate([
        _pad_groups(gmlp_w_in[..., :2 * GMLP_W], GMLP_GD, DV_PAD),
        gmlp_w_in[..., 2 * GMLP_W:],
    ], axis=-1).astype(BF16)
    w_tok = jnp.swapaxes(_pad_groups(jnp.swapaxes(w_out[:, :GMLP_W, :], 1, 2), GMLP_GD, DV_PAD), 1, 2).astype(BF16)
    w_mem = w_out[:, GMLP_W:, :].astype(BF16)
    return dict(
        norm_g=norm_g,
        ffn_in=ffn_w_in.astype(BF16), ffn_out=ffn_w_out.astype(BF16),
        ret_in=ret_in, ret_gn=_pad_groups(ret_gn, RET_DV, DV_PAD)[:, None, :],
        gmlp_in=gmlp_in,
        gmlp_ln=jnp.stack([_pad_groups(gmlp_ln_g, GMLP_GD, DV_PAD), _pad_groups(gmlp_ln_b, GMLP_GD, DV_PAD)], axis=1),
        gmlp_ws=gmlp_w_s, gmlp_bs=jnp.swapaxes(gmlp_b_s, 1, 2),
        w_tok=w_tok, w_mem=w_mem,
    )


def _run_trunk(x, pos, s_ret, mem_k, mem_v, w, want_v):
    b, t, _ = x.shape
    cos, sin = _rope_tables(pos)
    new_s, new_v = [], []
    for i in range(DEPTH):
        g = w['norm_g'][i]
        j = i // 2
        x = _ffn_call(x.reshape(b * t, D_MODEL), g[0:2], w['ffn_in'][i, 0], w['ffn_out'][i, 0]).reshape(b, t, D_MODEL)
        kbd, vbd = _block_diag_kv(mem_k[i], mem_v[i])
        if i % 2 == 0:
            x, s_fin = _ret_call(x, g[2:4], w['ret_in'][j], cos, sin, w['ret_gn'][j], _pad_state(s_ret[j]),
                                 kbd, vbd, w['w_tok'][i], w['w_mem'][i])
            new_s.append(_unpad_state(s_fin))
        else:
            chunk = min(GMLP_CHUNK, t)
            outs = _gmlp_call(x, g[2:4], w['gmlp_in'][j], w['gmlp_ln'][j], w['gmlp_ws'][j][:, :chunk, :chunk],
                              w['gmlp_bs'][j][:chunk], kbd, vbd, w['w_tok'][i], w['w_mem'][i], want_v)
            x = outs[0]
            if want_v:
                vp = outs[1]
                new_v.append(vp.reshape(b, t, GMLP_GROUPS, DV_PAD)[..., :GMLP_GD].reshape(b, t, GMLP_W))
        x = _ffn_call(x.reshape(b * t, D_MODEL), g[4:6], w['ffn_in'][i, 1], w['ffn_out'][i, 1]).reshape(b, t, D_MODEL)
    return x, new_s, new_v


def kernel(x_prompt, x_sample, state_ret, cache_mem_k, cache_mem_v, mem_prompt, norm_g, mem_norm, w_mem_kv,
           ffn_w_in, ffn_w_out, ret_w_in, ret_gn, gmlp_w_in, gmlp_ln_g, gmlp_ln_b, gmlp_w_s, gmlp_b_s, w_out):
    w = _prep_weights(norm_g, ffn_w_in, ffn_w_out, ret_w_in, ret_gn, gmlp_w_in, gmlp_ln_g, gmlp_ln_b,
                      gmlp_w_s, gmlp_b_s, w_out)
    bp, tp = x_prompt.shape[:2]
    k2d, v2d = _memkv_call(mem_prompt.reshape(bp * N_MEM, D_MODEL), mem_norm, w_mem_kv.astype(BF16))
    mem_k_p = k2d.reshape(DEPTH, bp, N_MEM, MEM_HEADS, MEM_DH)
    mem_v_p = v2d.reshape(DEPTH, bp, N_MEM, MEM_HEADS, MEM_DH)
    n_ret = state_ret.shape[0]
    s0 = jnp.zeros((n_ret, bp, RET_HEADS, RET_DK, RET_DV), F32)
    y_prompt, s_p, _ = _run_trunk(x_prompt, jnp.arange(tp, dtype=jnp.int32), s0, mem_k_p, mem_v_p, w, False)
    ts = x_sample.shape[1]
    y_sample, s_s, v_s = _run_trunk(x_sample, PAST_LEN + jnp.arange(ts, dtype=jnp.int32), state_ret,
                                    cache_mem_k, cache_mem_v, w, True)
    return (y_prompt, y_sample, jnp.stack(s_p), jnp.stack(s_s), jnp.stack(v_s), mem_k_p, mem_v_p)
```

```python
import functools
import math

import jax
import jax.numpy as jnp
from jax import lax
from jax.experimental import pallas as pl
from jax.experimental.pallas import tpu as pltpu

D_MODEL = 1024
DEPTH = 4
MEM_W = 256
MEM_HEADS = 4
MEM_DH = 64
N_MEM = 256
RET_HEADS = 4
RET_DK = 96
RET_DV = 192
ROPE_BASE = 10000.0
GMLP_W = 768
GMLP_GROUPS = 4
GMLP_GD = 192
GMLP_CHUNK = 128
D_FF = 2816
EPS = 1e-6
PAST_LEN = 4096

DK_PAD = 128
DV_PAD = 256
TOK_PAD = RET_HEADS * DV_PAD
HALF = RET_DK // 2
HALF_PAD = DK_PAD // 2
MXU_W = 256

FFN_TILE = 512
FFN_CHUNKS = (1536, 1280)
MIX_TILE = 512
RET_CHUNK = 256
VMEM_LIMIT = 52 * 1024 * 1024

BF16 = jnp.bfloat16
F32 = jnp.float32

assert sum(FFN_CHUNKS) == D_FF and all(c % MXU_W == 0 for c in FFN_CHUNKS)
assert MEM_DH ** -0.5 == 2.0 ** -3


def _dot(a, b):
    return jnp.dot(a, b, preferred_element_type=F32)


def _dot_nt(a, b):
    return lax.dot_general(a, b, (((1,), (1,)), ((), ())), preferred_element_type=F32)


def _dot_tn(a, b):
    return lax.dot_general(a, b, (((0,), (0,)), ((), ())), preferred_element_type=F32)


def _rms(x, g):
    return x * lax.rsqrt(jnp.mean(x * x, axis=-1, keepdims=True) + EPS) * g


def _silu(x):
    return x * (1.0 / (1.0 + jnp.exp(-x)))


def _const_spec(block, index):
    return pl.BlockSpec(block, lambda *_: index, pipeline_mode=pl.Buffered(1))


def _params(n_axes):
    return pltpu.CompilerParams(dimension_semantics=("arbitrary",) * n_axes, vmem_limit_bytes=VMEM_LIMIT)


def _ffn_kernel(x_ref, g_ref, win_ref, wout_ref, o_ref):
    x = x_ref[...]
    h = _rms(x, g_ref[0:1, :]).astype(BF16)
    acc = None
    lo = 0
    for fc in FFN_CHUNKS:
        gate = _dot(h, win_ref[:, lo:lo + fc])
        up = _dot(h, win_ref[:, D_FF + lo:D_FF + lo + fc])
        part = _dot((_silu(gate) * up).astype(BF16), wout_ref[lo:lo + fc, :])
        acc = part if acc is None else acc + part
        lo += fc
    o_ref[...] = x + 0.5 * _rms(acc, g_ref[1:2, :])


def _ffn_call(x2d, w, layer, which):
    n = x2d.shape[0]
    tm = min(FFN_TILE, n)
    assert n % tm == 0
    return pl.pallas_call(
        _ffn_kernel,
        grid=(n // tm,),
        in_specs=[
            pl.BlockSpec((tm, D_MODEL), lambda i: (i, 0)),
            _const_spec((None, None, 2, D_MODEL), (layer, 2 * which, 0, 0)),
            _const_spec((None, None, D_MODEL, 2 * D_FF), (layer, which, 0, 0)),
            _const_spec((None, None, D_FF, D_MODEL), (layer, which, 0, 0)),
        ],
        out_specs=pl.BlockSpec((tm, D_MODEL), lambda i: (i, 0)),
        out_shape=jax.ShapeDtypeStruct((n, D_MODEL), F32),
        compiler_params=_params(1),
        name="ffn",
    )(x2d, w['norm_g'], w['ffn_in'], w['ffn_out'])


def _build_block_diag(k_ref, v_ref, kbd_ref, vbd_ref):
    kt = k_ref[...].T * (MEM_DH ** -0.5)
    v = v_ref[...]
    k_head = lax.broadcasted_iota(jnp.int32, (MEM_W, N_MEM), 0) // MEM_DH
    v_head = lax.broadcasted_iota(jnp.int32, (N_MEM, MEM_W), 1) // MEM_DH
    for h in range(MEM_HEADS):
        kbd_ref[:, h * N_MEM:(h + 1) * N_MEM] = jnp.where(k_head == h, kt, 0.0).astype(BF16)
        vbd_ref[h * N_MEM:(h + 1) * N_MEM, :] = jnp.where(v_head == h, v, 0.0).astype(BF16)


def _memory_attend(xq, kbd_ref, vbd_ref):
    s = _dot(xq.astype(BF16), kbd_ref[...])
    ps = []
    for h in range(MEM_HEADS):
        sh = s[:, h * N_MEM:(h + 1) * N_MEM]
        e = jnp.exp(sh - jnp.max(sh, axis=-1, keepdims=True))
        ps.append((e * (1.0 / jnp.sum(e, axis=-1, keepdims=True))).astype(BF16))
    return _dot(jnp.concatenate(ps, axis=-1), vbd_ref[...])


def _mix_out(x, tok, mem_o, wtok_ref, wmem_ref, g_post):
    tb = tok.astype(BF16)
    mb = mem_o.astype(BF16)
    hw = D_MODEL // 2
    mix = jnp.concatenate([_dot(tb, wtok_ref[:, 0:hw]) + _dot(mb, wmem_ref[:, 0:hw]),
                           _dot(tb, wtok_ref[:, hw:]) + _dot(mb, wmem_ref[:, hw:])], axis=-1)
    return x + _rms(mix, g_post)


def _mixer_specs(tt, layer, w_in_block, w_in_index):
    head = [
        pl.BlockSpec((None, tt, D_MODEL), lambda i, j: (i, j, 0)),
        _const_spec((None, None, 2, D_MODEL), (layer, 1, 0, 0)),
        _const_spec(w_in_block, w_in_index),
    ]
    tail = [
        pl.BlockSpec((None, None, N_MEM, MEM_W), lambda i, j: (layer, i, 0, 0)),
        pl.BlockSpec((None, None, N_MEM, MEM_W), lambda i, j: (layer, i, 0, 0)),
        _const_spec((None, TOK_PAD, D_MODEL), (layer, 0, 0)),
        _const_spec((None, MEM_W, D_MODEL), (layer, 0, 0)),
    ]
    return head, tail


_KV_SCRATCH = [pltpu.VMEM((MEM_W, MEM_HEADS * N_MEM), BF16), pltpu.VMEM((MEM_HEADS * N_MEM, MEM_W), BF16)]


def _ret_kernel(x_ref, g_ref, win_ref, cos_ref, sin_ref, gn_ref, *rest, chunk, n_chunks, has_s0):
    if has_s0:
        s0_ref, rest = rest[0], rest[1:]
    k_ref, v_ref, wtok_ref, wmem_ref, o_ref, sout_ref, kbd_ref, vbd_ref, s_ref = rest
    t = pl.program_id(1)

    @pl.when(t == 0)
    def _():
        _build_block_diag(k_ref, v_ref, kbd_ref, vbd_ref)
        s_ref[...] = jnp.zeros_like(s_ref)
        if has_s0:
            for hd in range(RET_HEADS):
                s_ref[hd, 0:HALF, 0:RET_DV] = s0_ref[hd, 0:HALF, :]
                s_ref[hd, HALF_PAD:HALF_PAD + HALF, 0:RET_DV] = s0_ref[hd, HALF:RET_DK, :]

    x = x_ref[...]
    h = _rms(x, g_ref[0:1, :]).astype(BF16)
    ko, vo, go, xo = 4 * DK_PAD, 8 * DK_PAD, 8 * DK_PAD + TOK_PAD, 8 * DK_PAD + 2 * TOK_PAD
    p_qk = _dot(h, win_ref[:, 0:vo])
    p_v = _dot(h, win_ref[:, vo:go])
    p_g = _dot(h, win_ref[:, go:xo])
    p_x = _dot(h, win_ref[:, xo:xo + MEM_W])
    cos = cos_ref[...]
    sin = sin_ref[...]

    idx = lax.broadcasted_iota(jnp.int32, (chunk, 1), 0).astype(F32)
    row = lax.broadcasted_iota(jnp.int32, (chunk, chunk), 0)
    col = lax.broadcasted_iota(jnp.int32, (chunk, chunk), 1)
    diff = (row - col).astype(F32)

    o_heads = []
    for hd in range(RET_HEADS):
        log_g = math.log(1.0 - 2.0 ** (-5.0 - hd))
        decay = jnp.where(diff >= 0, jnp.exp(log_g * jnp.maximum(diff, 0.0)), 0.0)
        q_dec = jnp.exp(log_g * (idx + 1.0))
        k_dec = jnp.exp(log_g * (chunk - 1.0 - idx))
        s_dec = math.exp(log_g * chunk)
        qh = p_qk[:, hd * DK_PAD:(hd + 1) * DK_PAD]
        kh = p_qk[:, ko + hd * DK_PAD:ko + (hd + 1) * DK_PAD]
        qh = qh * cos + pltpu.roll(qh, HALF_PAD, 1) * sin
        kh = (kh * cos + pltpu.roll(kh, HALF_PAD, 1) * sin) * (RET_DK ** -0.5)
        vh = p_v[:, hd * DV_PAD:(hd + 1) * DV_PAD]
        o_chunks = []
        for c in range(n_chunks):
            sl = slice(c * chunk, (c + 1) * chunk)
            qc = qh[sl].astype(BF16)
            kc = kh[sl]
            vc = vh[sl].astype(BF16)
            st = s_ref[hd]
            att = (_dot_nt(qc, kc.astype(BF16)) * decay).astype(BF16)
            o_chunks.append(_dot(att, vc) + _dot(qc, st.astype(BF16)) * q_dec)
            s_ref[hd] = st * s_dec + _dot_tn((kc * k_dec).astype(BF16), vc)
        o = o_chunks[0] if n_chunks == 1 else jnp.concatenate(o_chunks, axis=0)
        ms = jnp.sum(o * o, axis=-1, keepdims=True) * (1.0 / RET_DV)
        o_heads.append(o * lax.rsqrt(ms + EPS) * gn_ref[:, hd * DV_PAD:(hd + 1) * DV_PAD])
    tok = _silu(p_g) * jnp.concatenate(o_heads, axis=-1)
    mem_o = _memory_attend(p_x, kbd_ref, vbd_ref)
    o_ref[...] = _mix_out(x, tok, mem_o, wtok_ref, wmem_ref, g_ref[1:2, :])

    @pl.when(t == pl.num_programs(1) - 1)
    def _():
        for hd in range(RET_HEADS):
            sout_ref[hd, 0:HALF, :] = s_ref[hd, 0:HALF, 0:RET_DV]
            sout_ref[hd, HALF:RET_DK, :] = s_ref[hd, HALF_PAD:HALF_PAD + HALF, 0:RET_DV]


def _ret_call(x, w, layer, cos, sin, s0, mem_k, mem_v):
    b, t, _ = x.shape
    j = layer // 2
    tt = min(MIX_TILE, t)
    chunk = min(RET_CHUNK, tt)
    n_in = w['ret_in'].shape[-1]
    has_s0 = s0 is not None
    kern = functools.partial(_ret_kernel, chunk=chunk, n_chunks=tt // chunk, has_s0=has_s0)
    head, tail = _mixer_specs(tt, layer, (None, D_MODEL, n_in), (j, 0, 0))
    state_spec = pl.BlockSpec((None, None, RET_HEADS, RET_DK, RET_DV), lambda i, jj: (j, i, 0, 0, 0))
    mid = [
        pl.BlockSpec((tt, DK_PAD), lambda i, jj: (jj, 0)),
        pl.BlockSpec((tt, DK_PAD), lambda i, jj: (jj, 0)),
        _const_spec((None, 1, TOK_PAD), (j, 0, 0)),
    ]
    args = [x, w['norm_g'], w['ret_in'], cos, sin, w['ret_gn']]
    if has_s0:
        mid.append(state_spec)
        args.append(s0)
    args += [mem_k, mem_v, w['w_tok'], w['w_mem']]
    return pl.pallas_call(
        kern,
        grid=(b, t // tt),
        in_specs=head + mid + tail,
        out_specs=[
            pl.BlockSpec((None, tt, D_MODEL), lambda i, jj: (i, jj, 0)),
            pl.BlockSpec((None, RET_HEADS, RET_DK, RET_DV), lambda i, jj: (i, 0, 0, 0)),
        ],
        out_shape=[
            jax.ShapeDtypeStruct((b, t, D_MODEL), F32),
            jax.ShapeDtypeStruct((b, RET_HEADS, RET_DK, RET_DV), F32),
        ],
        scratch_shapes=_KV_SCRATCH + [pltpu.VMEM((RET_HEADS, DK_PAD, DV_PAD), F32)],
        compiler_params=_params(2),
        name="retention_mixer",
    )(*args)


def _gelu(x):
    return 0.5 * x * (1.0 + lax.erf(x * (2.0 ** -0.5)))


def _gmlp_kernel(x_ref, g_ref, win_ref, ln_ref, ws_ref, bs_ref, k_ref, v_ref, wtok_ref, wmem_ref,
                 o_ref, *rest, chunk, n_chunks, want_v):
    if want_v:
        vout_ref, rest = rest[0], rest[1:]
    kbd_ref, vbd_ref = rest

    @pl.when(pl.program_id(1) == 0)
    def _():
        _build_block_diag(k_ref, v_ref, kbd_ref, vbd_ref)

    x = x_ref[...]
    h = _rms(x, g_ref[0:1, :]).astype(BF16)
    u = _gelu(_dot(h, win_ref[:, 0:TOK_PAD]))
    v = _gelu(_dot(h, win_ref[:, TOK_PAD:2 * TOK_PAD]))
    xq = _dot(h, win_ref[:, 2 * TOK_PAD:2 * TOK_PAD + MEM_W])
    lane = lax.broadcasted_iota(jnp.int32, (1, TOK_PAD), 1)
    real = (lane % DV_PAD) < GMLP_GD
    mu = jnp.sum(v, axis=-1, keepdims=True) * (1.0 / GMLP_W)
    vc = jnp.where(real, v - mu, 0.0)
    var = jnp.sum(vc * vc, axis=-1, keepdims=True) * (1.0 / GMLP_W)
    v = vc * lax.rsqrt(var + EPS) * ln_ref[0:1, :] + ln_ref[1:2, :]
    if want_v:
        for gi in range(GMLP_GROUPS):
            vout_ref[:, gi * GMLP_GD:(gi + 1) * GMLP_GD] = v[:, gi * DV_PAD:gi * DV_PAD + GMLP_GD]
    vb = v.astype(BF16)
    row = lax.broadcasted_iota(jnp.int32, (chunk, chunk), 0)
    col = lax.broadcasted_iota(jnp.int32, (chunk, chunk), 1)
    groups = []
    for gi in range(GMLP_GROUPS):
        wg = jnp.where(row >= col, ws_ref[gi, 0:chunk, 0:chunk], 0.0).astype(BF16)
        bias = bs_ref[0:chunk, gi:gi + 1]
        outs = []
        for c in range(n_chunks):
            vg = vb[c * chunk:(c + 1) * chunk, gi * DV_PAD:(gi + 1) * DV_PAD]
            outs.append(_dot(wg, vg) + bias)
        groups.append(outs[0] if n_chunks == 1 else jnp.concatenate(outs, axis=0))
    tok = u * jnp.concatenate(groups, axis=-1)
    mem_o = _memory_attend(xq, kbd_ref, vbd_ref)
    o_ref[...] = _mix_out(x, tok, mem_o, wtok_ref, wmem_ref, g_ref[1:2, :])


def _gmlp_call(x, w, layer, mem_k, mem_v, want_v):
    b, t, _ = x.shape
    j = layer // 2
    tt = min(MIX_TILE, t)
    chunk = min(GMLP_CHUNK, t)
    n_in = w['gmlp_in'].shape[-1]
    kern = functools.partial(_gmlp_kernel, chunk=chunk, n_chunks=tt // chunk, want_v=want_v)
    head, tail = _mixer_specs(tt, layer, (None, D_MODEL, n_in), (j, 0, 0))
    mid = [
        _const_spec((None, 2, TOK_PAD), (j, 0, 0)),
        _const_spec((None, GMLP_GROUPS, GMLP_CHUNK, GMLP_CHUNK), (j, 0, 0, 0)),
        _const_spec((None, GMLP_CHUNK, GMLP_GROUPS), (j, 0, 0)),
    ]
    out_specs = [pl.BlockSpec((None, tt, D_MODEL), lambda i, jj: (i, jj, 0))]
    out_shape = [jax.ShapeDtypeStruct((b, t, D_MODEL), F32)]
    if want_v:
        out_specs.append(pl.BlockSpec((None, tt, GMLP_W), lambda i, jj: (i, jj, 0)))
        out_shape.append(jax.ShapeDtypeStruct((b, t, GMLP_W), F32))
    return pl.pallas_call(
        kern,
        grid=(b, t // tt),
        in_specs=head + mid + tail,
        out_specs=out_specs,
        out_shape=out_shape,
        scratch_shapes=_KV_SCRATCH,
        compiler_params=_params(2),
        name="gmlp_mixer",
    )(x, w['norm_g'], w['gmlp_in'], w['gmlp_ln'], w['gmlp_ws'], w['gmlp_bs'], mem_k, mem_v,
      w['w_tok'], w['w_mem'])


def _memkv_kernel(mem_ref, norm_ref, w_ref, k_ref, v_ref):
    m = mem_ref[...]
    mhat = m * lax.rsqrt(jnp.mean(m * m, axis=-1, keepdims=True) + EPS)
    kv = _dot((mhat * norm_ref[...]).astype(BF16), w_ref[...])
    k_ref[...] = kv[:, :MEM_W]
    v_ref[...] = kv[:, MEM_W:]


def _memkv_call(mem2d, mem_norm, w_kv):
    n = mem2d.shape[0]
    return pl.pallas_call(
        _memkv_kernel,
        grid=(DEPTH,),
        in_specs=[
            _const_spec((n, D_MODEL), (0, 0)),
            pl.BlockSpec((None, 1, D_MODEL), lambda l: (l, 0, 0)),
            pl.BlockSpec((None, D_MODEL, 2 * MEM_W), lambda l: (l, 0, 0)),
        ],
        out_specs=[pl.BlockSpec((None, n, MEM_W), lambda l: (l, 0, 0))] * 2,
        out_shape=[jax.ShapeDtypeStruct((DEPTH, n, MEM_W), F32)] * 2,
        compiler_params=_params(1),
        name="memory_kv",
    )(mem2d, mem_norm.reshape(DEPTH, 1, D_MODEL), w_kv)


def _pad_last(a, n):
    return jnp.pad(a, [(0, 0)] * (a.ndim - 1) + [(0, n - a.shape[-1])])


def _pad_groups(a, real, padded):
    lead = a.shape[:-1]
    g = a.shape[-1] // real
    return _pad_last(a.reshape(lead + (g, real)), padded).reshape(lead + (g * padded,))


def _pad_qk(a):
    lead = a.shape[:-1]
    h = a.shape[-1] // RET_DK
    return _pad_last(a.reshape(lead + (h, 2, HALF)), HALF_PAD).reshape(lead + (h * DK_PAD,))


def _rope_tables(pos):
    inv = 1.0 / (ROPE_BASE ** (jnp.arange(HALF, dtype=F32) / HALF))
    ang = pos.astype(F32)[:, None] * inv[None, :]
    cos = _pad_last(jnp.cos(ang), HALF_PAD)
    sin = _pad_last(jnp.sin(ang), HALF_PAD)
    return jnp.concatenate([cos, cos], axis=-1), jnp.concatenate([-sin, sin], axis=-1)


def _prep_weights(norm_g, ffn_w_in, ffn_w_out, ret_w_in, ret_gn, gmlp_w_in, gmlp_ln_g, gmlp_ln_b,
                  gmlp_w_s, gmlp_b_s, w_out):
    qk = 2 * RET_HEADS * RET_DK
    vw = RET_HEADS * RET_DV
    ret_in = jnp.concatenate([
        _pad_qk(ret_w_in[..., :qk]),
        _pad_groups(ret_w_in[..., qk:qk + 2 * vw], RET_DV, DV_PAD),
        ret_w_in[..., qk + 2 * vw:],
    ], axis=-1).astype(BF16)
    gmlp_in = jnp.concatenate([
        _pad_groups(gmlp_w_in[..., :2 * GMLP_W], GMLP_GD, DV_PAD),
        gmlp_w_in[..., 2 * GMLP_W:],
    ], axis=-1).astype(BF16)
    w_tok = w_out[:, :GMLP_W, :].reshape(DEPTH, GMLP_GROUPS, GMLP_GD, D_MODEL)
    w_tok = jnp.pad(w_tok, [(0, 0), (0, 0), (0, DV_PAD - GMLP_GD), (0, 0)]).reshape(DEPTH, TOK_PAD, D_MODEL)
    return dict(
        norm_g=norm_g.reshape(DEPTH, 3, 2, D_MODEL),
        ffn_in=ffn_w_in.astype(BF16), ffn_out=ffn_w_out.astype(BF16),
        ret_in=ret_in, ret_gn=_pad_groups(ret_gn, RET_DV, DV_PAD)[:, None, :],
        gmlp_in=gmlp_in,
        gmlp_ln=jnp.stack([_pad_groups(gmlp_ln_g, GMLP_GD, DV_PAD), _pad_groups(gmlp_ln_b, GMLP_GD, DV_PAD)], axis=1),
        gmlp_ws=gmlp_w_s, gmlp_bs=jnp.swapaxes(gmlp_b_s, 1, 2),
        w_tok=w_tok.astype(BF16), w_mem=w_out[:, GMLP_W:, :].astype(BF16),
    )


def _run_trunk(x, pos, s_ret, mem_k, mem_v, w, want_v):
    b, t, _ = x.shape
    cos, sin = _rope_tables(pos)
    new_s, new_v = [], []

    def ffn(x, layer, which):
        return _ffn_call(x.reshape(b * t, D_MODEL), w, layer, which).reshape(b, t, D_MODEL)

    for i in range(DEPTH):
        x = ffn(x, i, 0)
        if i % 2 == 0:
            x, s_fin = _ret_call(x, w, i, cos, sin, s_ret, mem_k, mem_v)
            new_s.append(s_fin)
        else:
            outs = _gmlp_call(x, w, i, mem_k, mem_v, want_v)
            x = outs[0]
            if want_v:
                new_v.append(outs[1])
        x = ffn(x, i, 1)
    return x, new_s, new_v


def kernel(x_prompt, x_sample, state_ret, cache_mem_k, cache_mem_v, mem_prompt, norm_g, mem_norm, w_mem_kv,
           ffn_w_in, ffn_w_out, ret_w_in, ret_gn, gmlp_w_in, gmlp_ln_g, gmlp_ln_b, gmlp_w_s, gmlp_b_s, w_out):
    w = _prep_weights(norm_g, ffn_w_in, ffn_w_out, ret_w_in, ret_gn, gmlp_w_in, gmlp_ln_g, gmlp_ln_b,
                      gmlp_w_s, gmlp_b_s, w_out)
    bp, tp = x_prompt.shape[:2]
    bs, ts = x_sample.shape[:2]
    k2d, v2d = _memkv_call(mem_prompt.reshape(bp * N_MEM, D_MODEL), mem_norm, w_mem_kv.astype(BF16))
    y_prompt, s_p, _ = _run_trunk(x_prompt, jnp.arange(tp, dtype=jnp.int32), None,
                                  k2d.reshape(DEPTH, bp, N_MEM, MEM_W), v2d.reshape(DEPTH, bp, N_MEM, MEM_W),
                                  w, False)
    y_sample, s_s, v_s = _run_trunk(x_sample, PAST_LEN + jnp.arange(ts, dtype=jnp.int32), state_ret,
                                    cache_mem_k.reshape(DEPTH, bs, N_MEM, MEM_W),
                                    cache_mem_v.reshape(DEPTH, bs, N_MEM, MEM_W), w, True)
    mem_shape = (DEPTH, bp, N_MEM, MEM_HEADS, MEM_DH)
    return (y_prompt, y_sample, jnp.stack(s_p), jnp.stack(s_s), jnp.stack(v_s),
            k2d.reshape(mem_shape), v2d.reshape(mem_shape))
```
